```python
import jax, jax.numpy as jnp
from jax import lax
import numpy as np

D_MODEL = 1024
BATCH = 4
SEQ = 4096
DEPTH = 1
DEC_BATCH = 128
DEC_SEQ = 4
PAST_LEN = 8192
PAGE_SIZE = 128

FOX_HEADS = 8
FOX_HD = 64
FOX_W = FOX_HEADS * FOX_HD
Q_BLOCK = 128
GLA_HEADS = 4
GLA_DK = 64
GLA_DV = 128
GLA_KW = GLA_HEADS * GLA_DK
GLA_VW = GLA_HEADS * GLA_DV
GLA_GATE_RANK = 16
GLA_TAU = 16.0
GLA_CHUNK = 64
D_MIX = FOX_W + GLA_VW
IN_SIZES = (FOX_W, FOX_W, FOX_W, FOX_HEADS, GLA_KW, GLA_KW, GLA_VW, GLA_GATE_RANK, GLA_VW)
IN_OFFSETS = tuple(int(v) for v in np.cumsum(IN_SIZES)[:-1])
D_IN = sum(IN_SIZES)
N_MEM = 256
MEM_HEADS = 4
MEM_HD = D_MODEL // MEM_HEADS
D_FF = ((8 * D_MODEL // 3 + 127) // 128) * 128
CONV_W = 3
EPS = 1e-6
CACHE_FORGET_BIAS = 7.0

kernel_name = 'hymba_fox_gla_convffn_memxattn_step'


def rmsnorm(x, g):
    xf = x.astype(jnp.float32)
    y = xf * lax.rsqrt(jnp.mean(xf * xf, axis=-1, keepdims=True) + EPS)
    return (y * g.astype(jnp.float32)).astype(x.dtype)


def mixer_proj(h, w_in, b_f, w_gk2, b_gk):
    b, l, _ = h.shape
    proj = h @ w_in
    fq, fk, fv, fz, gq, gk, gv, g1, gr = jnp.split(proj, IN_OFFSETS, axis=-1)
    logf = jax.nn.log_sigmoid((fz + b_f).astype(jnp.float32))
    glog = jax.nn.log_sigmoid((g1 @ w_gk2 + b_gk).astype(jnp.float32)) / GLA_TAU
    return (fq.reshape(b, l, FOX_HEADS, FOX_HD), fk.reshape(b, l, FOX_HEADS, FOX_HD),
            fv.reshape(b, l, FOX_HEADS, FOX_HD), logf,
            gq.reshape(b, l, GLA_HEADS, GLA_DK), gk.reshape(b, l, GLA_HEADS, GLA_DK),
            gv.reshape(b, l, GLA_HEADS, GLA_DV), glog.reshape(b, l, GLA_HEADS, GLA_DK), gr)


def fox_prompt(q, k, v, logf):
    b, s, h, d = q.shape
    c = jnp.cumsum(logf, axis=1).transpose(0, 2, 1)
    kpos = jnp.arange(s)
    scale = d ** -0.5

    def block(i):
        s0 = i * Q_BLOCK
        qb = lax.dynamic_slice_in_dim(q, s0, Q_BLOCK, axis=1)
        cq = lax.dynamic_slice_in_dim(c, s0, Q_BLOCK, axis=2)
        logits = jnp.einsum('bqhd,bkhd->bhqk', qb, k).astype(jnp.float32) * scale
        logits = logits + cq[..., :, None] - c[..., None, :]
        qpos = s0 + jnp.arange(Q_BLOCK)
        logits = jnp.where(kpos[None, :] <= qpos[:, None], logits, -jnp.inf)
        p = jax.nn.softmax(logits, axis=-1).astype(v.dtype)
        return jnp.einsum('bhqk,bkhd->bqhd', p, v)

    out = lax.map(block, jnp.arange(s // Q_BLOCK))
    return out.transpose(1, 0, 2, 3, 4).reshape(b, s, h * d)


def fox_sample(q, k_new, v_new, lf_new, k_past, v_past, lf_past):
    b, t, h, d = q.shape
    scale = d ** -0.5
    lf_past = lf_past.astype(jnp.float32)
    suffix = (lax.cumsum(lf_past, axis=1, reverse=True) - lf_past).transpose(0, 2, 1)
    cn = jnp.cumsum(lf_new.astype(jnp.float32), axis=1).transpose(0, 2, 1)
    lp = jnp.einsum('bqhd,bkhd->bhqk', q, k_past).astype(jnp.float32) * scale
    lp = lp + cn[..., :, None] + suffix[..., None, :]
    ln = jnp.einsum('bqhd,bkhd->bhqk', q, k_new).astype(jnp.float32) * scale
    ln = ln + cn[..., :, None] - cn[..., None, :]
    ln = jnp.where(jnp.tril(jnp.ones((t, t), dtype=bool)), ln, -jnp.inf)
    p = jax.nn.softmax(jnp.concatenate([lp, ln], axis=-1), axis=-1).astype(v_new.dtype)
    n_past = k_past.shape[1]
    o = (jnp.einsum('bhqk,bkhd->bqhd', p[..., :n_past], v_past)
         + jnp.einsum('bhqk,bkhd->bqhd', p[..., n_past:], v_new))
    return o.reshape(b, t, h * d)


def gla_chunked(q, k, v, g, s0):
    b, l, h, dk = q.shape
    dv = v.shape[-1]
    c = GLA_CHUNK if l % GLA_CHUNK == 0 else l
    n = l // c

    def to_chunks(a):
        return a.astype(jnp.float32).reshape(b, n, c, h, a.shape[-1]).transpose(1, 0, 3, 2, 4)

    qc = to_chunks(q) * (dk ** -0.5)
    kc, vc, gc = to_chunks(k), to_chunks(v), to_chunks(g)
    mask = jnp.tril(jnp.ones((c, c), dtype=bool))

    def step(state, xs):
        qi, ki, vi, gi = xs
        cum = jnp.cumsum(gi, axis=2)
        q_t = qi * jnp.exp(cum)
        a = jnp.einsum('bhtd,bhsd->bhts', q_t, ki * jnp.exp(-cum))
        a = jnp.where(mask, a, 0.0)
        o = jnp.einsum('bhtd,bhdv->bhtv', q_t, state) + jnp.einsum('bhts,bhsv->bhtv', a, vi)
        last = cum[:, :, -1, :]
        state = (jnp.exp(last)[..., None] * state
                 + jnp.einsum('bhsd,bhsv->bhdv', ki * jnp.exp(last[:, :, None, :] - cum), vi))
        return state, o

    state, o = lax.scan(step, s0.astype(jnp.float32), (qc, kc, vc, gc))
    return o.transpose(1, 0, 3, 2, 4).reshape(b, l, h, dv), state


def gla_out(o, r, g):
    b, l, h, dv = o.shape
    on = o * lax.rsqrt(jnp.mean(o * o, axis=-1, keepdims=True) + EPS) * g.astype(jnp.float32).reshape(h, dv)
    return (on.reshape(b, l, h * dv) * jax.nn.silu(r.astype(jnp.float32))).astype(r.dtype)


def mem_kv(mem, w_ck, w_cv):
    b, n, _ = mem.shape
    return ((mem @ w_ck).reshape(b, n, MEM_HEADS, MEM_HD), (mem @ w_cv).reshape(b, n, MEM_HEADS, MEM_HD))


def cross_attn(h, mk, mv, w_cq, w_co):
    b, l, _ = h.shape
    q = (h @ w_cq).reshape(b, l, MEM_HEADS, MEM_HD)
    logits = jnp.einsum('blhd,bnhd->bhln', q, mk).astype(jnp.float32) * (MEM_HD ** -0.5)
    p = jax.nn.softmax(logits, axis=-1).astype(mv.dtype)
    o = jnp.einsum('bhln,bnhd->blhd', p, mv).reshape(b, l, MEM_HEADS * MEM_HD)
    return o @ w_co


def conv_ffn(h, conv_state, w_up, conv_w, conv_b, w_down):
    u = h @ w_up
    l = u.shape[1]
    up = jnp.concatenate([conv_state.astype(u.dtype), u], axis=1)
    y = conv_b + sum(conv_w[j] * up[:, j:j + l] for j in range(CONV_W))
    gate, val = jnp.split(y, 2, axis=-1)
    return (jax.nn.silu(gate) * val) @ w_down, up[:, -(CONV_W - 1):]


def setup_inputs(seed: int = 0) -> dict:
    key = jax.random.key(seed)
    ks = list(jax.random.split(key, 40))
    it = iter(ks)

    def nrm(shape, scale=1.0):
        return jax.random.normal(next(it), shape, jnp.float32) * scale

    n_pages = PAST_LEN // PAGE_SIZE
    n_pool = (DEC_BATCH * n_pages * 5) // 4
    f2 = 2 * D_FF
    x_prompt = nrm((BATCH, SEQ, D_MODEL))
    x_sample = nrm((DEC_BATCH, DEC_SEQ, D_MODEL))
    cache_fox_k = nrm((DEPTH, n_pool, PAGE_SIZE, FOX_HEADS, FOX_HD))
    cache_fox_v = nrm((DEPTH, n_pool, PAGE_SIZE, FOX_HEADS, FOX_HD))
    cache_fox_logf = jax.nn.log_sigmoid(CACHE_FORGET_BIAS + nrm((DEPTH, n_pool, PAGE_SIZE, FOX_HEADS), 0.5))
    page_table = jax.random.permutation(next(it), n_pool)[:DEC_BATCH * n_pages].reshape(DEC_BATCH, n_pages).astype(jnp.int32)
    state_gla = nrm((DEPTH, DEC_BATCH, GLA_HEADS, GLA_DK, GLA_DV))
    state_ffn_conv = nrm((DEPTH, DEC_BATCH, CONV_W - 1, f2))
    cache_mem_k = nrm((DEPTH, DEC_BATCH, N_MEM, MEM_HEADS, MEM_HD))
    cache_mem_v = nrm((DEPTH, DEC_BATCH, N_MEM, MEM_HEADS, MEM_HD))
    mem_prompt = nrm((BATCH, N_MEM, D_MODEL))
    return {
        'x_prompt': x_prompt,
        'x_sample': x_sample,
        'cache_fox_k': cache_fox_k,
        'cache_fox_v': cache_fox_v,
        'cache_fox_logf': cache_fox_logf,
        'page_table': page_table,
        'state_gla': state_gla,
        'state_ffn_conv': state_ffn_conv,
        'cache_mem_k': cache_mem_k,
        'cache_mem_v': cache_mem_v,
        'mem_prompt': mem_prompt,
        'norm_mix': 1.0 + nrm((DEPTH, D_MODEL), 0.02),
        'w_in': nrm((DEPTH, D_MODEL, D_IN), D_MODEL ** -0.5),
        'b_f': 3.0 + nrm((DEPTH, FOX_HEADS), 0.1),
        'w_gk2': nrm((DEPTH, GLA_GATE_RANK, GLA_KW), GLA_GATE_RANK ** -0.5),
        'b_gk': nrm((DEPTH, GLA_KW), 0.1),
        'gla_norm': 1.0 + nrm((DEPTH, GLA_VW), 0.02),
        'w_out': nrm((DEPTH, D_MIX, D_MODEL), D_MIX ** -0.5),
        'norm_cross': 1.0 + nrm((DEPTH, D_MODEL), 0.02),
        'w_cq': nrm((DEPTH, D_MODEL, MEM_HEADS * MEM_HD), D_MODEL ** -0.5),
        'w_ck': nrm((DEPTH, D_MODEL, MEM_HEADS * MEM_HD), D_MODEL ** -0.5),
        'w_cv': nrm((DEPTH, D_MODEL, MEM_HEADS * MEM_HD), D_MODEL ** -0.5),
        'w_co': nrm((DEPTH, MEM_HEADS * MEM_HD, D_MODEL), (MEM_HEADS * MEM_HD) ** -0.5),
        'norm_ffn': 1.0 + nrm((DEPTH, D_MODEL), 0.02),
        'w_up': nrm((DEPTH, D_MODEL, f2), D_MODEL ** -0.5),
        'conv_w': nrm((DEPTH, CONV_W, f2), CONV_W ** -0.5),
        'conv_b': nrm((DEPTH, f2), 0.02),
        'w_down': nrm((DEPTH, D_FF, D_MODEL), D_FF ** -0.5),
        'norm_final': 1.0 + nrm((D_MODEL,), 0.02),
    }


def reference(x_prompt, x_sample, cache_fox_k, cache_fox_v, cache_fox_logf, page_table, state_gla,
              state_ffn_conv, cache_mem_k, cache_mem_v, mem_prompt, norm_mix, w_in, b_f, w_gk2, b_gk,
              gla_norm, w_out, norm_cross, w_cq, w_ck, w_cv, w_co, norm_ffn, w_up, conv_w, conv_b,
              w_down, norm_final):
    db, n_pages = page_table.shape
    n_past = n_pages * PAGE_SIZE
    bp = x_prompt.shape[0]
    xp, xs = x_prompt, x_sample
    fkp, fvp, flp, fks, fvs, fls = [], [], [], [], [], []
    gsp, gss, csp, css, mkp, mvp = [], [], [], [], [], []
    for l in range(DEPTH):
        hp = rmsnorm(xp, norm_mix[l])
        fq, fk, fv, lf, gq, gk, gv, gg, gr = mixer_proj(hp, w_in[l], b_f[l], w_gk2[l], b_gk[l])
        fox_o = fox_prompt(fq, fk, fv, lf)
        go, gst = gla_chunked(gq, gk, gv, gg, jnp.zeros((bp, GLA_HEADS, GLA_DK, GLA_DV), jnp.float32))
        xp = xp + jnp.concatenate([fox_o, gla_out(go, gr, gla_norm[l])], axis=-1) @ w_out[l]
        fkp.append(fk); fvp.append(fv); flp.append(lf); gsp.append(gst)
        hs = rmsnorm(xs, norm_mix[l])
        sq, sk, sv, slf, sgq, sgk, sgv, sgg, sgr = mixer_proj(hs, w_in[l], b_f[l], w_gk2[l], b_gk[l])
        k_past = cache_fox_k[l][page_table].reshape(db, n_past, FOX_HEADS, FOX_HD)
        v_past = cache_fox_v[l][page_table].reshape(db, n_past, FOX_HEADS, FOX_HD)
        lf_past = cache_fox_logf[l][page_table].reshape(db, n_past, FOX_HEADS)
        fox_s = fox_sample(sq, sk, sv, slf, k_past, v_past, lf_past)
        gos, gst_s = gla_chunked(sgq, sgk, sgv, sgg, state_gla[l])
        xs = xs + jnp.concatenate([fox_s, gla_out(gos, sgr, gla_norm[l])], axis=-1) @ w_out[l]
        fks.append(sk); fvs.append(sv); fls.append(slf); gss.append(gst_s)
        mk, mv = mem_kv(mem_prompt, w_ck[l], w_cv[l])
        xp = xp + cross_attn(rmsnorm(xp, norm_cross[l]), mk, mv, w_cq[l], w_co[l])
        xs = xs + cross_attn(rmsnorm(xs, norm_cross[l]), cache_mem_k[l], cache_mem_v[l], w_cq[l], w_co[l])
        mkp.append(mk); mvp.append(mv)
        zero_conv = jnp.zeros((bp, CONV_W - 1, 2 * D_FF), xp.dtype)
        fo, cst = conv_ffn(rmsnorm(xp, norm_ffn[l]), zero_conv, w_up[l], conv_w[l], conv_b[l], w_down[l])
        xp = xp + fo
        fo_s, cst_s = conv_ffn(rmsnorm(xs, norm_ffn[l]), state_ffn_conv[l], w_up[l], conv_w[l], conv_b[l], w_down[l])
        xs = xs + fo_s
        csp.append(cst); css.append(cst_s)
    y_prompt = rmsnorm(xp, norm_final)
    y_sample = rmsnorm(xs, norm_final)
    return (y_prompt, y_sample,
            jnp.stack(fkp), jnp.stack(fvp), jnp.stack(flp),
            jnp.stack(fks), jnp.stack(fvs), jnp.stack(fls),
            jnp.stack(gsp), jnp.stack(gss),
            jnp.stack(csp), jnp.stack(css),
            jnp.stack(mkp), jnp.stack(mvp))
```

```python
import functools

import jax
import jax.numpy as jnp
from jax import lax
from jax.experimental import pallas as pl
from jax.experimental.pallas import tpu as pltpu

F32 = jnp.float32
BF16 = jnp.bfloat16

EPS = 1e-6
FOX_HEADS = 8
FOX_HD = 64
GLA_HEADS = 4
GLA_DK = 64
GLA_DV = 128
GLA_GATE_RANK = 16
GLA_TAU = 16.0
GLA_CHUNK = 64
MEM_HEADS = 4
CONV_W = 3
PAGE_SIZE = 128

FOX_W = FOX_HEADS * FOX_HD
GLA_KW = GLA_HEADS * GLA_DK
GLA_VW = GLA_HEADS * GLA_DV
LANES = 128
SUBLANES = 8
VMEM_LIMIT = 56 * 1024 * 1024
NEG = -1e30

LF_LANE = 0
G1_LANE = 8
MID_LANE = 24
LO_LANE = 32


def _cparams(sem, vmem=VMEM_LIMIT):
    return pltpu.CompilerParams(dimension_semantics=sem, vmem_limit_bytes=vmem)


def _const_spec(shape):
    nd = len(shape)
    return pl.BlockSpec(shape, lambda *_: (0,) * nd, pipeline_mode=pl.Buffered(1))


def _log_sigmoid(z):
    return jnp.minimum(z, 0.0) - jnp.log1p(jnp.exp(-jnp.abs(z)))


def _silu(z):
    return z * (1.0 / (1.0 + jnp.exp(-z)))


def _rms(x, g):
    return x * lax.rsqrt(jnp.mean(x * x, axis=-1, keepdims=True) + EPS) * g


def _split3(x):
    hi = x.astype(BF16).astype(F32)
    r1 = x - hi
    mid = r1.astype(BF16).astype(F32)
    lo = (r1 - mid).astype(BF16).astype(F32)
    return hi, mid, lo


def _dot(a, b):
    return jnp.dot(a, b, preferred_element_type=F32)


def _dot_nt(a, b):
    return lax.dot_general(a, b, (((1,), (1,)), ((), ())), preferred_element_type=F32)


C_QKV = 0
C_G = 3 * FOX_W
C_R = C_G + 2 * GLA_KW + GLA_VW
C_S = C_R + GLA_VW
C_END = C_S + LANES


def _inproj_body(x_ref, g_ref, w_ref, bsm_ref, w2_ref, bgk_ref, tri_ref,
                 q_ref, kx_ref, vb_ref, fk_ref, fv_ref, lf_ref, c_ref,
                 gq_ref, gk_ref, gl_ref, gv_ref, gr_ref, carry_ref, *, tiles_per_seg):
    i = pl.program_id(0)
    tm = x_ref.shape[0]
    h = _rms(x_ref[...], g_ref[...]).astype(BF16)

    r = _dot(h, w_ref[:, C_QKV:C_G])
    q_ref[...] = (r[:, 0:FOX_W] * (FOX_HD ** -0.5)).astype(BF16)
    fk = r[:, FOX_W:2 * FOX_W]
    fv = r[:, 2 * FOX_W:3 * FOX_W]
    fk_ref[...] = fk
    fv_ref[...] = fv
    vb_ref[...] = fv.astype(BF16)
    kb = fk.astype(BF16)

    rg = _dot(h, w_ref[:, C_G:C_R])
    gq_ref[...] = rg[:, 0:GLA_KW]
    gk_ref[...] = rg[:, GLA_KW:2 * GLA_KW]
    gv_ref[...] = rg[:, 2 * GLA_KW:]
    gr_ref[...] = _dot(h, w_ref[:, C_R:C_S])

    sm = _dot(h, w_ref[:, C_S:C_END])
    lf = _log_sigmoid(sm + bsm_ref[...])
    lf_ref[...] = lf[:, LF_LANE:LF_LANE + FOX_HEADS]
    gl = _dot(sm.astype(BF16), w2_ref[...]) + bgk_ref[...]
    gl_ref[...] = _log_sigmoid(gl) * (1.0 / GLA_TAU)

    parts = jnp.concatenate(_split3(lf), axis=1).astype(BF16)
    cs = _dot(tri_ref[...], parts)
    c = cs[:, 0:LANES] + cs[:, LANES:2 * LANES] + cs[:, 2 * LANES:]
    if tiles_per_seg:
        @pl.when(i % tiles_per_seg == 0)
        def _():
            carry_ref[...] = jnp.zeros_like(carry_ref)
        c = c + carry_ref[0:1, :]
        carry_ref[...] = jnp.broadcast_to(c[tm - 1:tm, :], carry_ref.shape)
    c_ref[...] = c

    chi, cmid, clo = _split3(c)
    lane = lax.broadcasted_iota(jnp.int32, c.shape, 1)
    cpart = jnp.where(lane < FOX_HEADS, chi,
                      jnp.where((lane >= MID_LANE) & (lane < MID_LANE + FOX_HEADS), cmid,
                                jnp.where((lane >= LO_LANE) & (lane < LO_LANE + FOX_HEADS), clo, 0.0)))
    cpart = cpart.astype(BF16)
    pieces = []
    for j in range(FOX_HEADS // 2):
        pieces += [kb[:, j * LANES:(j + 1) * LANES], cpart]
    kx_ref[...] = jnp.concatenate(pieces, axis=1)


def _inproj(x, g, wp, bsm, w2p, bgk, *, seg_len):
    n, d = x.shape
    tm = min(512, n)
    assert n % tm == 0
    if seg_len >= tm:
        assert seg_len % tm == 0
        tiles_per_seg = seg_len // tm
        seg = tm
    else:
        assert tm % seg_len == 0
        tiles_per_seg = 0
        seg = seg_len
    r = jnp.arange(tm)
    tri = ((r[:, None] >= r[None, :]) & (r[:, None] // seg == r[None, :] // seg)).astype(BF16)

    row = lambda w: pl.BlockSpec((tm, w), lambda i: (i, 0))
    outs = [
        (FOX_W, BF16),
        (2 * FOX_W, BF16),
        (FOX_W, BF16),
        (FOX_W, F32),
        (FOX_W, F32),
        (FOX_HEADS, F32),
        (LANES, F32),
        (GLA_KW, F32),
        (GLA_KW, F32),
        (GLA_KW, F32),
        (GLA_VW, F32),
        (GLA_VW, F32),
    ]
    return pl.pallas_call(
        functools.partial(_inproj_body, tiles_per_seg=tiles_per_seg),
        grid=(n // tm,),
        in_specs=[row(d), _const_spec((1, d)), _const_spec(wp.shape), _const_spec((1, LANES)),
                  _const_spec(w2p.shape), _const_spec((1, GLA_KW)), _const_spec((tm, tm))],
        out_specs=[row(w) for w, _ in outs],
        out_shape=[jax.ShapeDtypeStruct((n, w), dt) for w, dt in outs],
        scratch_shapes=[pltpu.VMEM((SUBLANES, LANES), F32)],
        compiler_params=_cparams(("arbitrary",)),
        name="inproj",
    )(x, g, wp, bsm, w2p, bgk, tri)


def _fox_prompt_body(q_ref, kx_ref, v_ref, o_ref, m_ref, l_ref, acc_ref, *, tq):
    j = pl.program_id(1)
    i = pl.program_id(2)
    q2 = q_ref[...]
    lane = lax.broadcasted_iota(jnp.int32, (tq, LANES), 1)
    low = lane < FOX_HD
    zero = jnp.zeros_like(q2)
    qx = []
    for hd in range(2):
        h = 2 * j + hd
        qm = jnp.where(low if hd == 0 else ~low, q2, zero)
        sel = (lane == h + LF_LANE) | (lane == h + MID_LANE) | (lane == h + LO_LANE)
        e = jnp.where(sel, -1.0, 0.0).astype(BF16)
        qx.append(jnp.concatenate([qm, e], axis=1))

    m_ref[...] = jnp.full_like(m_ref, NEG)
    l_ref[...] = jnp.zeros_like(l_ref)
    acc_ref[...] = jnp.zeros_like(acc_ref)
    row = lax.broadcasted_iota(jnp.int32, (tq, tq), 0)
    col = lax.broadcasted_iota(jnp.int32, (tq, tq), 1)
    causal = col <= row

    def chunk(c, masked):
        start = pl.multiple_of(c * tq, tq)
        kc = kx_ref[pl.ds(start, tq), :]
        vc = v_ref[pl.ds(start, tq), :]
        alphas, pvs = [], []
        for hd in range(2):
            s = _dot_nt(qx[hd], kc)
            if masked:
                s = jnp.where(causal, s, NEG)
            m_prev = m_ref[hd]
            m_next = jnp.maximum(m_prev, jnp.max(s, axis=1, keepdims=True))
            alpha = jnp.exp(m_prev - m_next)
            p = jnp.exp(s - m_next)
            l_ref[hd] = alpha * l_ref[hd] + jnp.sum(p, axis=1, keepdims=True)
            m_ref[hd] = m_next
            alphas.append(alpha)
            pvs.append(_dot(p.astype(BF16), vc))
        alpha2 = jnp.where(low, alphas[0], alphas[1])
        pv2 = jnp.where(low, pvs[0], pvs[1])
        acc_ref[...] = acc_ref[...] * alpha2 + pv2

    def body(c, carry):
        chunk(c, False)
        return carry

    lax.fori_loop(0, i, body, 0)
    chunk(i, True)
    l2 = jnp.where(low, l_ref[0], l_ref[1])
    o_ref[...] = (acc_ref[...] / l2).astype(o_ref.dtype)


def _fox_prompt(q, kx, v, *, batch, seq):
    n = q.shape[0]
    tq = min(256, seq)
    nq = seq // tq
    npair = FOX_HEADS // 2
    return pl.pallas_call(
        functools.partial(_fox_prompt_body, tq=tq),
        grid=(batch, npair, nq),
        in_specs=[pl.BlockSpec((tq, LANES), lambda b, j, i: (b * nq + i, j)),
                  pl.BlockSpec((seq, 2 * LANES), lambda b, j, i: (b, j)),
                  pl.BlockSpec((seq, LANES), lambda b, j, i: (b, j))],
        out_specs=pl.BlockSpec((tq, LANES), lambda b, j, i: (b * nq + i, j)),
        out_shape=jax.ShapeDtypeStruct((n, FOX_W), BF16),
        scratch_shapes=[pltpu.VMEM((2, tq, 1), F32), pltpu.VMEM((2, tq, 1), F32),
                        pltpu.VMEM((tq, LANES), F32)],
        compiler_params=_cparams(("arbitrary", "arbitrary", "arbitrary")),
        name="fox_prompt",
    )(q, kx, v)


def _gla_body(gq_ref, gk_ref, gl_ref, gv_ref, gr_ref, gn_ref, s0_ref, tri_ref,
              o_ref, st_ref, s2_ref, *, chunk):
    i = pl.program_id(2)
    t = gq_ref.shape[0]
    dk2 = 2 * GLA_DK
    dv2 = 2 * GLA_DV

    @pl.when(i == 0)
    def _():
        z = jnp.zeros((GLA_DK, GLA_DV), F32)
        top = jnp.concatenate([s0_ref[0, 0], z], axis=1)
        bot = jnp.concatenate([z, s0_ref[0, 1]], axis=1)
        s2_ref[...] = jnp.concatenate([top, bot], axis=0)

    lane = lax.broadcasted_iota(jnp.int32, (chunk, dk2), 1)
    low = lane < GLA_DK
    r = lax.broadcasted_iota(jnp.int32, (chunk, chunk), 0)
    c = lax.broadcasted_iota(jnp.int32, (chunk, chunk), 1)
    causal = c <= r
    srow = lax.broadcasted_iota(jnp.int32, (dk2, dv2), 0)
    scol = lax.broadcasted_iota(jnp.int32, (dk2, dv2), 1)
    blockdiag = (srow < GLA_DK) == (scol < GLA_DV)
    tri = tri_ref[...]

    outs = []
    for ci in range(t // chunk):
        rows = slice(ci * chunk, (ci + 1) * chunk)
        g = gl_ref[rows, :]
        q = gq_ref[rows, :]
        k = gk_ref[rows, :]
        v = gv_ref[rows, :].astype(BF16)
        parts = jnp.concatenate(_split3(g), axis=1)
        cs = _dot(tri, parts)
        cum = cs[:, 0:dk2] + cs[:, dk2:2 * dk2] + cs[:, 2 * dk2:]
        last = cum[chunk - 1:chunk, :]
        qt = q * jnp.exp(cum) * (GLA_DK ** -0.5)
        kn = (k * jnp.exp(-cum)).astype(BF16)
        kd = k * jnp.exp(last - cum)
        state = s2_ref[...]
        o = _dot(qt.astype(BF16), state.astype(BF16))
        zero = jnp.zeros_like(qt)
        intra = []
        for hd in range(2):
            qm = jnp.where(low if hd == 0 else ~low, qt, zero).astype(BF16)
            a = jnp.where(causal, _dot_nt(qm, kn), 0.0).astype(BF16)
            intra.append(_dot(a, v[:, hd * GLA_DV:(hd + 1) * GLA_DV]))
        outs.append(o + jnp.concatenate(intra, axis=1))
        ext = jnp.concatenate([kd, jnp.broadcast_to(jnp.exp(last), (SUBLANES, dk2))], axis=0)
        ext_t = ext.T
        upd = _dot(ext_t[:, 0:chunk].astype(BF16), v)
        s2_ref[...] = ext_t[:, chunk:chunk + 1] * state + jnp.where(blockdiag, upd, 0.0)

    o_all = outs[0] if len(outs) == 1 else jnp.concatenate(outs, axis=0)
    gate = gr_ref[...]
    normed = []
    for hd in range(2):
        oh = o_all[:, hd * GLA_DV:(hd + 1) * GLA_DV]
        normed.append(oh * lax.rsqrt(jnp.mean(oh * oh, axis=-1, keepdims=True) + EPS))
    on = jnp.concatenate(normed, axis=1) * gn_ref[...]
    o_ref[...] = (on * _silu(gate)).astype(o_ref.dtype)

    @pl.when(i == pl.num_programs(2) - 1)
    def _():
        st_ref[0, 0] = s2_ref[0:GLA_DK, 0:GLA_DV]
        st_ref[0, 1] = s2_ref[GLA_DK:, GLA_DV:]


def _gla(gq, gk, gl, gv, gr, gn, s0, *, batch, seq, chunk, out_dtype):
    n = gq.shape[0]
    t = min(256, seq)
    assert seq % t == 0 and t % chunk == 0
    nt = seq // t
    npair = GLA_HEADS // 2
    r = jnp.arange(chunk)
    tri = (r[:, None] >= r[None, :]).astype(F32)
    tok = lambda w: pl.BlockSpec((t, w), lambda b, j, i: (b * nt + i, j))
    st_spec = pl.BlockSpec((1, 2, GLA_DK, GLA_DV), lambda b, j, i: (b, j, 0, 0))
    return pl.pallas_call(
        functools.partial(_gla_body, chunk=chunk),
        grid=(batch, npair, nt),
        in_specs=[tok(2 * GLA_DK), tok(2 * GLA_DK), tok(2 * GLA_DK), tok(2 * GLA_DV), tok(2 * GLA_DV),
                  pl.BlockSpec((1, 2 * GLA_DV), lambda b, j, i: (0, j)), st_spec,
                  pl.BlockSpec((chunk, chunk), lambda b, j, i: (0, 0))],
        out_specs=[tok(2 * GLA_DV), st_spec],
        out_shape=[jax.ShapeDtypeStruct((n, GLA_VW), out_dtype),
                   jax.ShapeDtypeStruct(s0.shape, F32)],
        scratch_shapes=[pltpu.VMEM((2 * GLA_DK, 2 * GLA_DV), F32)],
        compiler_params=_cparams(("arbitrary", "arbitrary", "arbitrary")),
        name="gla",
    )(gq, gk, gl, gv, gr, gn, s0, tri)


def _mid_body(x_ref, fo_ref, gm_ref, wo_ref, nc_ref, wq_ref, x1_ref, qc_ref, *, qscale):
    half = fo_ref.shape[1]
    x1 = (x_ref[...] + _dot(fo_ref[...].astype(BF16), wo_ref[0:half, :])
          + _dot(gm_ref[...].astype(BF16), wo_ref[half:, :]))
    x1_ref[...] = x1
    hc = _rms(x1, nc_ref[...]).astype(BF16)
    qc_ref[...] = (_dot(hc, wq_ref[...]) * qscale).astype(qc_ref.dtype)


def _mid(x, fo, gm, wo, nc, wq, *, qscale):
    n, d = x.shape
    tm = min(512, n)
    row = lambda a: pl.BlockSpec((tm, a.shape[1]), lambda i: (i, 0))
    return pl.pallas_call(
        functools.partial(_mid_body, qscale=qscale),
        grid=(n // tm,),
        in_specs=[row(x), row(fo), row(gm), _const_spec(wo.shape), _const_spec((1, d)), _const_spec(wq.shape)],
        out_specs=[pl.BlockSpec((tm, d), lambda i: (i, 0)), pl.BlockSpec((tm, wq.shape[1]), lambda i: (i, 0))],
        out_shape=[jax.ShapeDtypeStruct((n, d), F32), jax.ShapeDtypeStruct((n, wq.shape[1]), BF16)],
        compiler_params=_cparams(("arbitrary",)),
        name="mid",
    )(x, fo, gm, wo, nc, wq)


def _memkv_body(m_ref, w_ref, k_ref, v_ref, kb_ref, vb_ref):
    r = _dot(m_ref[...].astype(BF16), w_ref[...])
    w = k_ref.shape[1]
    k_ref[...] = r[:, 0:w]
    v_ref[...] = r[:, w:]
    kb_ref[...] = r[:, 0:w].astype(BF16)
    vb_ref[...] = r[:, w:].astype(BF16)


def _memkv(mem, wkv):
    n, d = mem.shape
    w = wkv.shape[1] // 2
    tm = min(256, n)
    row = lambda width: pl.BlockSpec((tm, width), lambda i: (i, 0))
    return pl.pallas_call(
        _memkv_body,
        grid=(n // tm,),
        in_specs=[row(d), _const_spec(wkv.shape)],
        out_specs=[row(w), row(w), row(w), row(w)],
        out_shape=[jax.ShapeDtypeStruct((n, w), F32), jax.ShapeDtypeStruct((n, w), F32),
                   jax.ShapeDtypeStruct((n, w), BF16), jax.ShapeDtypeStruct((n, w), BF16)],
        compiler_params=_cparams(("arbitrary",)),
        name="memkv",
    )(mem, wkv)


def _xattn_heads(q, k, v, hd):
    outs = []
    for h in range(q.shape[1] // hd):
        sl = slice(h * hd, (h + 1) * hd)
        s = _dot_nt(q[:, sl], k[:, sl])
        p = jnp.exp(s - jnp.max(s, axis=1, keepdims=True))
        o = _dot(p.astype(BF16), v[:, sl])
        outs.append(o / jnp.sum(p, axis=1, keepdims=True))
    return jnp.concatenate(outs, axis=1)


def _xattn_prompt_body(q_ref, k_ref, v_ref, o_ref, *, hd):
    o_ref[...] = _xattn_heads(q_ref[...], k_ref[...], v_ref[...], hd).astype(o_ref.dtype)


def _xattn_prompt(qc, mkb, mvb, *, batch, seq, n_mem):
    n, w = qc.shape
    tq = min(512, seq)
    nq = seq // tq
    return pl.pallas_call(
        functools.partial(_xattn_prompt_body, hd=w // MEM_HEADS),
        grid=(batch, nq),
        in_specs=[pl.BlockSpec((tq, w), lambda b, i: (b * nq + i, 0)),
                  pl.BlockSpec((n_mem, w), lambda b, i: (b, 0)),
                  pl.BlockSpec((n_mem, w), lambda b, i: (b, 0))],
        out_specs=pl.BlockSpec((tq, w), lambda b, i: (b * nq + i, 0)),
        out_shape=jax.ShapeDtypeStruct((n, w), BF16),
        compiler_params=_cparams(("arbitrary", "arbitrary")),
        name="xattn_prompt",
    )(qc, mkb, mvb)


def _xattn_sample_body(q_ref, k_ref, v_ref, o_ref, *, hd):
    o_ref[0] = _xattn_heads(q_ref[0].astype(BF16), k_ref[0].astype(BF16), v_ref[0].astype(BF16), hd)


def _xattn_sample(qpad, mk, mv):
    db, tp, w = qpad.shape
    n_mem = mk.shape[1]
    return pl.pallas_call(
        functools.partial(_xattn_sample_body, hd=w // MEM_HEADS),
        grid=(db,),
        in_specs=[pl.BlockSpec((1, tp, w), lambda b: (b, 0, 0)),
                  pl.BlockSpec((1, n_mem, w), lambda b: (b, 0, 0)),
                  pl.BlockSpec((1, n_mem, w), lambda b: (b, 0, 0))],
        out_specs=pl.BlockSpec((1, tp, w), lambda b: (b, 0, 0)),
        out_shape=jax.ShapeDtypeStruct((db, tp, w), F32),
        compiler_params=_cparams(("arbitrary",)),
        name="xattn_sample",
    )(qpad, mk, mv)


FFN_CHUNK = 256


def _ffn_body(*refs, d_ff, tiles_per_seq, seg_rows):
    if seg_rows:
        (x_ref, oc_ref, wco_ref, nf_ref, wup_ref, cw_ref, cb_ref, wdn_ref, nfin_ref, st1_ref, st2_ref,
         y_ref, u_ref, ubuf_ref) = refs
    else:
        (x_ref, oc_ref, wco_ref, nf_ref, wup_ref, cw_ref, cb_ref, wdn_ref, nfin_ref,
         y_ref, cst_ref, ubuf_ref, carry_ref) = refs
    i = pl.program_id(0)
    tm = x_ref.shape[0]
    fc = FFN_CHUNK
    x2 = x_ref[...] + _dot(oc_ref[...].astype(BF16), wco_ref[...])
    hf = _rms(x2, nf_ref[...]).astype(BF16)

    if seg_rows:
        tpos = lax.broadcasted_iota(jnp.int32, (tm, 1), 0) % seg_rows
        ubuf_ref[0:SUBLANES, :] = jnp.zeros((SUBLANES, fc), F32)
    else:
        @pl.when(i % tiles_per_seq == 0)
        def _():
            carry_ref[...] = jnp.zeros_like(carry_ref)

    def conv(u, cols):
        ubuf_ref[SUBLANES:SUBLANES + tm, :] = u
        if seg_rows:
            u_ref[:, cols] = u
            p1 = jnp.where(tpos >= 1, ubuf_ref[pl.ds(SUBLANES - 1, tm), :], st1_ref[:, cols])
            p2 = jnp.where(tpos >= 2, ubuf_ref[pl.ds(SUBLANES - 2, tm), :], st2_ref[:, cols])
        else:
            ubuf_ref[0:SUBLANES, :] = carry_ref[:, cols]
            p1 = ubuf_ref[pl.ds(SUBLANES - 1, tm), :]
            p2 = ubuf_ref[pl.ds(SUBLANES - 2, tm), :]
            carry_ref[:, cols] = u[tm - SUBLANES:tm, :]
        return (cb_ref[:, cols] + cw_ref[2:3, cols] * u + cw_ref[1:2, cols] * p1 + cw_ref[0:1, cols] * p2)

    acc = jnp.zeros((tm, x_ref.shape[1]), F32)
    for c in range(d_ff // fc):
        gcols = slice(c * fc, (c + 1) * fc)
        vcols = slice(d_ff + c * fc, d_ff + (c + 1) * fc)
        yg = conv(_dot(hf, wup_ref[:, gcols]), gcols)
        yv = conv(_dot(hf, wup_ref[:, vcols]), vcols)
        act = (_silu(yg) * yv).astype(BF16)
        acc = acc + _dot(act, wdn_ref[gcols, :])
    y_ref[...] = _rms(x2 + acc, nfin_ref[...])
    if not seg_rows:
        cst_ref[0] = carry_ref[...]


def _ffn(x1, oc, wco, nf, wup, cw, cb, wdn, nfin, *, seq, state_rows=None):
    n, d = x1.shape
    d_ff = wdn.shape[0]
    assert d_ff % FFN_CHUNK == 0
    tm = min(512 if state_rows is None else 128, n)
    row = lambda a: pl.BlockSpec((tm, a.shape[1]), lambda i: (i, 0))
    common = [row(x1), row(oc), _const_spec(wco.shape), _const_spec((1, d)), _const_spec(wup.shape),
              _const_spec(cw.shape), _const_spec(cb.shape), _const_spec(wdn.shape), _const_spec((1, d))]
    ubuf = pltpu.VMEM((tm + SUBLANES, FFN_CHUNK), F32)
    if state_rows is None:
        assert seq % tm == 0
        tiles_per_seq = seq // tm
        return pl.pallas_call(
            functools.partial(_ffn_body, d_ff=d_ff, tiles_per_seq=tiles_per_seq, seg_rows=0),
            grid=(n // tm,),
            in_specs=common,
            out_specs=[pl.BlockSpec((tm, d), lambda i: (i, 0)),
                       pl.BlockSpec((1, SUBLANES, 2 * d_ff), lambda i: (i // tiles_per_seq, 0, 0))],
            out_shape=[jax.ShapeDtypeStruct((n, d), F32),
                       jax.ShapeDtypeStruct((n // seq, SUBLANES, 2 * d_ff), F32)],
            scratch_shapes=[ubuf, pltpu.VMEM((SUBLANES, 2 * d_ff), F32)],
            compiler_params=_cparams(("arbitrary",)),
            name="ffn_prompt",
        )(x1, oc, wco, nf, wup, cw, cb, wdn, nfin)
    st1, st2 = state_rows
    assert tm % seq == 0
    return pl.pallas_call(
        functools.partial(_ffn_body, d_ff=d_ff, tiles_per_seq=0, seg_rows=seq),
        grid=(n // tm,),
        in_specs=common + [row(st1), row(st2)],
        out_specs=[pl.BlockSpec((tm, d), lambda i: (i, 0)), pl.BlockSpec((tm, 2 * d_ff), lambda i: (i, 0))],
        out_shape=[jax.ShapeDtypeStruct((n, d), F32), jax.ShapeDtypeStruct((n, 2 * d_ff), F32)],
        scratch_shapes=[ubuf],
        compiler_params=_cparams(("arbitrary",)),
        name="ffn_sample",
    )(x1, oc, wco, nf, wup, cw, cb, wdn, nfin, st1, st2)


PAGES_PER_STEP = 16


def _fox_sample_body(pt_ref, q_ref, kn_ref, vn_ref, cn_ref, ltri_ref, cmat_ref, k_hbm, v_hbm, l_hbm,
                     o_ref, kbuf, vbuf, lbuf, sem, m_ref, l_ref, acc_ref, sc_ref, *, n_steps, n_new):
    b = pl.program_id(0)
    st = pl.program_id(1)
    pp = lbuf.shape[1]
    tok = pp * PAGE_SIZE
    g = b * n_steps + st
    slot = g % 2
    total = pl.num_programs(0) * n_steps

    def copies(bb, stt, sl):
        first = (n_steps - 1 - stt) * pp
        out = []
        for p in range(pp):
            pid = pt_ref[bb, first + p]
            lanes = pl.ds(p * PAGE_SIZE, PAGE_SIZE)
            out.append(pltpu.make_async_copy(k_hbm.at[pid], kbuf.at[sl, :, :, lanes], sem.at[0, sl]))
            out.append(pltpu.make_async_copy(v_hbm.at[pid], vbuf.at[sl, :, :, lanes], sem.at[1, sl]))
            out.append(pltpu.make_async_copy(l_hbm.at[pid], lbuf.at[sl, p], sem.at[2, sl]))
        return out

    @pl.when(g == 0)
    def _():
        for cp in copies(b, st, slot):
            cp.start()

    @pl.when(g + 1 < total)
    def _():
        wrap = st == n_steps - 1
        for cp in copies(jnp.where(wrap, b + 1, b), jnp.where(wrap, 0, st + 1), 1 - slot):
            cp.start()

    for cp in copies(b, st, slot):
        cp.wait()

    @pl.when(st == 0)
    def _():
        m_ref[...] = jnp.full_like(m_ref, NEG)
        l_ref[...] = jnp.zeros_like(l_ref)
        acc_ref[...] = jnp.zeros_like(acc_ref)
        sc_ref[...] = jnp.zeros_like(sc_ref)

    p8 = pp * FOX_HEADS
    lf2 = lbuf[slot].reshape(p8, PAGE_SIZE)
    parts = jnp.concatenate(_split3(lf2), axis=0).astype(BF16)
    r = _dot(parts, ltri_ref[...])
    r = r[0:p8] + r[p8:2 * p8] + r[2 * p8:]
    within = r[:, 0:PAGE_SIZE]
    tot = r[:, PAGE_SIZE:]
    tparts = jnp.concatenate(_split3(tot), axis=1).astype(BF16)
    cr = _dot(cmat_ref[...], tparts)
    cr = cr[:, 0:LANES] + cr[:, LANES:2 * LANES] + cr[:, 2 * LANES:]
    sc = sc_ref[...]
    bias = within + cr[0:p8] + jnp.concatenate([sc] * pp, axis=0)
    sc_ref[...] = sc + cr[p8:p8 + FOX_HEADS]

    q = q_ref[0]

    def update(h, s, v_apply):
        m_prev = m_ref[h]
        m_next = jnp.maximum(m_prev, jnp.max(s, axis=1, keepdims=True))
        alpha = jnp.exp(m_prev - m_next)
        p = jnp.exp(s - m_next)
        l_ref[h] = alpha * l_ref[h] + jnp.sum(p, axis=1, keepdims=True)
        m_ref[h] = m_next
        acc_ref[h] = alpha * acc_ref[h] + v_apply(p.astype(BF16))

    for h in range(FOX_HEADS):
        qh = q[:, h * FOX_HD:(h + 1) * FOX_HD].astype(BF16)
        s = _dot(qh, kbuf[slot, h].astype(BF16))
        bias_h = jnp.concatenate(
            [jnp.broadcast_to(bias[p * FOX_HEADS + h:p * FOX_HEADS + h + 1, :], (SUBLANES, PAGE_SIZE))
             for p in range(pp)], axis=1)
        vt = vbuf[slot, h].astype(BF16)
        update(h, s + bias_h, lambda pb, vt=vt: _dot_nt(vt, pb).T)

    @pl.when(st == n_steps - 1)
    def _():
        kn = kn_ref[0]
        vn = vn_ref[0]
        cn_t = cn_ref[0].T
        row = lax.broadcasted_iota(jnp.int32, (SUBLANES, SUBLANES), 0)
        col = lax.broadcasted_iota(jnp.int32, (SUBLANES, SUBLANES), 1)
        ok = (col <= row) & (col < n_new)
        outs = []
        for h in range(FOX_HEADS):
            sl = slice(h * FOX_HD, (h + 1) * FOX_HD)
            s = _dot_nt(q[:, sl].astype(BF16), kn[:, sl].astype(BF16)) - cn_t[h:h + 1, :]
            vh = vn[:, sl].astype(BF16)
            update(h, jnp.where(ok, s, NEG), lambda pb, vh=vh: _dot(pb, vh))
            outs.append(acc_ref[h] / l_ref[h])
        o_ref[0] = jnp.concatenate(outs, axis=1)


def _fox_sample(page_table, q, kn, vn, cn, k_t, v_t, l_t, *, n_new):
    db, n_pages = page_table.shape
    pp = min(PAGES_PER_STEP, n_pages)
    assert n_pages % pp == 0
    n_steps = n_pages // pp
    p8 = pp * FOX_HEADS
    tok = pp * PAGE_SIZE
    j = jnp.arange(PAGE_SIZE)
    ltri = jnp.concatenate([(j[:, None] > j[None, :]), jnp.ones((PAGE_SIZE, PAGE_SIZE), bool)],
                           axis=1).astype(BF16)
    r = jnp.arange(p8 + 2 * SUBLANES)
    c = jnp.arange(p8)
    same_head = (r[:, None] % FOX_HEADS) == (c[None, :] % FOX_HEADS)
    later = (c[None, :] // FOX_HEADS) > (r[:, None] // FOX_HEADS)
    cmat = jnp.where(r[:, None] < p8, same_head & later,
                     same_head & (r[:, None] < p8 + FOX_HEADS)).astype(BF16)
    blk = lambda a: pl.BlockSpec((1,) + a.shape[1:], lambda b, s, pt: (b, 0, 0))
    const = lambda a: pl.BlockSpec(a.shape, lambda b, s, pt: (0, 0))
    anyspec = pl.BlockSpec(memory_space=pl.ANY)
    grid_spec = pltpu.PrefetchScalarGridSpec(
        num_scalar_prefetch=1,
        grid=(db, n_steps),
        in_specs=[blk(q), blk(kn), blk(vn), blk(cn), const(ltri), const(cmat), anyspec, anyspec, anyspec],
        out_specs=pl.BlockSpec((1, SUBLANES, FOX_W), lambda b, s, pt: (b, 0, 0)),
        scratch_shapes=[pltpu.VMEM((2, FOX_HEADS, FOX_HD, tok), F32),
                        pltpu.VMEM((2, FOX_HEADS, FOX_HD, tok), F32),
                        pltpu.VMEM((2, pp, FOX_HEADS, PAGE_SIZE), F32),
                        pltpu.SemaphoreType.DMA((3, 2)),
                        pltpu.VMEM((FOX_HEADS, SUBLANES, 1), F32),
                        pltpu.VMEM((FOX_HEADS, SUBLANES, 1), F32),
                        pltpu.VMEM((FOX_HEADS, SUBLANES, FOX_HD), F32),
                        pltpu.VMEM((FOX_HEADS, LANES), F32)])
    return pl.pallas_call(
        functools.partial(_fox_sample_body, n_steps=n_steps, n_new=n_new),
        grid_spec=grid_spec,
        out_shape=jax.ShapeDtypeStruct((db, SUBLANES, FOX_W), F32),
        compiler_params=_cparams(("arbitrary", "arbitrary")),
        name="fox_sample",
    )(page_table, q, kn, vn, cn, ltri, cmat, k_t, v_t, l_t)


def _pad_rows(a, rows):
    return jnp.pad(a, ((0, 0), (0, rows - a.shape[1]), (0, 0)))


def kernel(x_prompt, x_sample, cache_fox_k, cache_fox_v, cache_fox_logf, page_table, state_gla,
           state_ffn_conv, cache_mem_k, cache_mem_v, mem_prompt, norm_mix, w_in, b_f, w_gk2, b_gk,
           gla_norm, w_out, norm_cross, w_cq, w_ck, w_cv, w_co, norm_ffn, w_up, conv_w, conv_b,
           w_down, norm_final):
    bp, seq, d = x_prompt.shape
    db, ts, _ = x_sample.shape
    depth = w_in.shape[0]
    n_mem = mem_prompt.shape[1]
    d_ff = w_down.shape[1]
    assert CONV_W - 1 <= ts <= SUBLANES
    assert depth == 1
    xp = x_prompt.reshape(bp * seq, d)
    xs = x_sample.reshape(db * ts, d)
    mem = mem_prompt.reshape(bp * n_mem, d)
    nfin = norm_final.reshape(1, d)
    outs = [[] for _ in range(12)]

    for l in range(depth):
        sizes = (FOX_W, FOX_W, FOX_W, FOX_HEADS, GLA_KW, GLA_KW, GLA_VW, GLA_GATE_RANK, GLA_VW)
        offs = [0]
        for s_ in sizes:
            offs.append(offs[-1] + s_)
        col = lambda a, k: a[..., offs[k]:offs[k + 1]]
        wl = w_in[l]
        pad = LANES - (LO_LANE + FOX_HEADS)
        wsmall = jnp.concatenate([col(wl, 3), col(wl, 7), col(wl, 3), col(wl, 3),
                                  jnp.zeros((d, pad), wl.dtype)], axis=1)
        wp = jnp.concatenate([col(wl, 0), col(wl, 1), col(wl, 2), col(wl, 4), col(wl, 5), col(wl, 6),
                              col(wl, 8), wsmall], axis=1).astype(BF16)
        bfl = b_f[l]
        bsm = jnp.concatenate([bfl, jnp.zeros((GLA_GATE_RANK,), F32), bfl, bfl,
                               jnp.zeros((pad,), F32)]).reshape(1, LANES)
        w2p = jnp.zeros((LANES, GLA_KW), F32).at[G1_LANE:G1_LANE + GLA_GATE_RANK].set(w_gk2[l]).astype(BF16)
        bgk = b_gk[l].reshape(1, GLA_KW)
        gmix = norm_mix[l].reshape(1, d)
        gn = gla_norm[l].reshape(1, GLA_VW)
        wo = w_out[l].astype(BF16)
        wq = w_cq[l].astype(BF16)
        wkv = jnp.concatenate([w_ck[l], w_cv[l]], axis=1).astype(BF16)
        wco = w_co[l].astype(BF16)
        wup = w_up[l].astype(BF16)
        wdn = w_down[l].astype(BF16)
        cw = conv_w[l]
        cb = conv_b[l].reshape(1, 2 * d_ff)
        ncr = norm_cross[l].reshape(1, d)
        nff = norm_ffn[l].reshape(1, d)
        qscale = (w_cq.shape[2] // MEM_HEADS) ** -0.5

        (q_p, kx_p, v_p, fk_p, fv_p, lf_p, _, gq_p, gk_p, gl_p, gv_p, gr_p) = _inproj(
            xp, gmix, wp, bsm, w2p, bgk, seg_len=seq)
        fo_p = _fox_prompt(q_p, kx_p, v_p, batch=bp, seq=seq)
        gm_p, gst_p = _gla(gq_p, gk_p, gl_p, gv_p, gr_p, gn,
                           jnp.zeros((bp, GLA_HEADS, GLA_DK, GLA_DV), F32),
                           batch=bp, seq=seq, chunk=GLA_CHUNK if seq % GLA_CHUNK == 0 else seq, out_dtype=BF16)
        x1_p, qc_p = _mid(xp, fo_p, gm_p, wo, ncr, wq, qscale=qscale)

        (q_s, _, _, fk_s, fv_s, lf_s, c_s, gq_s, gk_s, gl_s, gv_s, gr_s) = _inproj(
            xs, gmix, wp, bsm, w2p, bgk, seg_len=ts)
        pad8 = lambda a: _pad_rows(a.reshape(db, ts, a.shape[-1]), SUBLANES)
        k_t = cache_fox_k[l].transpose(0, 2, 3, 1)
        v_t = cache_fox_v[l].transpose(0, 2, 3, 1)
        l_t = cache_fox_logf[l].transpose(0, 2, 1)
        fo_s = _fox_sample(page_table, pad8(q_s.astype(F32)), pad8(fk_s), pad8(fv_s), pad8(c_s),
                           k_t, v_t, l_t, n_new=ts)
        fo_s = fo_s[:, :ts].reshape(db * ts, FOX_W)
        flat8 = lambda a: pad8(a).reshape(db * SUBLANES, a.shape[-1])
        gm_s, gst_s = _gla(flat8(gq_s), flat8(gk_s), flat8(gl_s), flat8(gv_s), flat8(gr_s), gn, state_gla[l],
                           batch=db, seq=SUBLANES, chunk=SUBLANES, out_dtype=F32)
        gm_s = gm_s.reshape(db, SUBLANES, GLA_VW)[:, :ts].reshape(db * ts, GLA_VW)
        x1_s, qc_s = _mid(xs, fo_s, gm_s, wo, ncr, wq, qscale=qscale)

        mk, mv, mkb, mvb = _memkv(mem, wkv)
        oc_p = _xattn_prompt(qc_p, mkb, mvb, batch=bp, seq=seq, n_mem=n_mem)
        oc_s = _xattn_sample(pad8(qc_s.astype(F32)), cache_mem_k[l].reshape(db, n_mem, -1),
                             cache_mem_v[l].reshape(db, n_mem, -1))
        oc_s = oc_s[:, :ts].reshape(db * ts, -1)

        xp, cst_p = _ffn(x1_p, oc_p, wco, nff, wup, cw, cb, wdn, nfin, seq=seq)
        stc = state_ffn_conv[l]
        zrow = jnp.zeros((db, 1, 2 * d_ff), F32)
        st1 = jnp.concatenate([stc[:, 1:2]] + [zrow] * (ts - 1), axis=1).reshape(db * ts, 2 * d_ff)
        st2 = jnp.concatenate([stc[:, 0:1], stc[:, 1:2]] + [zrow] * (ts - 2), axis=1).reshape(db * ts, 2 * d_ff)
        xs, u_s = _ffn(x1_s, oc_s, wco, nff, wup, cw, cb, wdn, nfin, seq=ts, state_rows=(st1, st2))

        mh = w_ck.shape[2] // MEM_HEADS
        vals = (fk_p.reshape(bp, seq, FOX_HEADS, FOX_HD), fv_p.reshape(bp, seq, FOX_HEADS, FOX_HD),
                lf_p.reshape(bp, seq, FOX_HEADS),
                fk_s.reshape(db, ts, FOX_HEADS, FOX_HD), fv_s.reshape(db, ts, FOX_HEADS, FOX_HD),
                lf_s.reshape(db, ts, FOX_HEADS),
                gst_p, gst_s,
                cst_p[:, SUBLANES - (CONV_W - 1):], u_s.reshape(db, ts, 2 * d_ff)[:, ts - (CONV_W - 1):],
                mk.reshape(bp, n_mem, MEM_HEADS, mh), mv.reshape(bp, n_mem, MEM_HEADS, mh))
        for lst, v_ in zip(outs, vals):
            lst.append(v_)

    return (xp.reshape(bp, seq, d), xs.reshape(db, ts, d)) + tuple(jnp.stack(o) for o in outs)
```
